```python
import jax, jax.numpy as jnp
from jax import lax
import numpy as np

D_MODEL = 4096
BATCH = 4
SEQ = 4096
DEPTH = 2

MIX_WIDTH = D_MODEL
RET_HEADS = 16
SB_HEADS = 16
HEAD_DIM = MIX_WIDTH // (RET_HEADS + SB_HEADS)
RET_WIDTH = RET_HEADS * HEAD_DIM
SB_WIDTH = SB_HEADS * HEAD_DIM
IN_WIDTH = 4 * RET_WIDTH + 3 * SB_WIDTH
D_FF = 2 * D_MODEL
RET_CHUNK = 128
SB_BLOCK = 128
ROPE_BASE = 10000.0
N_MOD = 9
EPS = 1e-6

kernel_name = "hybrid_retention_stickbreaking_macaron_adaln"


def rmsnorm(x, g):
    xf = x.astype(jnp.float32)
    y = xf * lax.rsqrt(jnp.mean(xf * xf, axis=-1, keepdims=True) + EPS)
    return (y * g.astype(jnp.float32)).astype(x.dtype)


def modulate(h, shift, scale):
    return h * (1.0 + scale[:, None, :]) + shift[:, None, :]


def swiglu(h, w_gu, w_down):
    gate, up = jnp.split(h @ w_gu, 2, axis=-1)
    return (jax.nn.silu(gate) * up) @ w_down


def rope(x):
    S, d = x.shape[1], x.shape[3]
    half = d // 2
    inv = ROPE_BASE ** (-jnp.arange(half, dtype=jnp.float32) / half)
    ang = jnp.arange(S, dtype=jnp.float32)[:, None] * inv[None, :]
    cos = jnp.cos(ang)[None, :, None, :]
    sin = jnp.sin(ang)[None, :, None, :]
    xf = x.astype(jnp.float32)
    x1, x2 = xf[..., :half], xf[..., half:]
    out = jnp.concatenate([x1 * cos - x2 * sin, x1 * sin + x2 * cos], axis=-1)
    return out.astype(x.dtype)


def retention(q, k, v):
    B, H, S, dk = q.shape
    dv = v.shape[-1]
    C = RET_CHUNK
    n = S // C
    dt = q.dtype
    lg = jnp.log1p(-jnp.exp2(-5.0 - jnp.arange(H, dtype=jnp.float32)))
    idx = jnp.arange(C, dtype=jnp.float32)
    diff = idx[:, None] - idx[None, :]
    dmask = jnp.where(diff >= 0, jnp.exp(lg[:, None, None] * jnp.maximum(diff, 0.0)), 0.0).astype(dt)
    xi = jnp.exp(lg[:, None] * (idx[None, :] + 1.0)).astype(dt)
    zeta = jnp.exp(lg[:, None] * (C - 1.0 - idx[None, :])).astype(dt)
    chunk_decay = jnp.exp(lg * C).astype(dt)
    k = k * (dk ** -0.5)
    qc = q.reshape(B, H, n, C, dk)
    kc = k.reshape(B, H, n, C, dk)
    vc = v.reshape(B, H, n, C, dv)
    scores = jnp.einsum('bhnid,bhnjd->bhnij', qc, kc) * dmask[None, :, None]
    inner = jnp.einsum('bhnij,bhnjv->bhniv', scores, vc)
    kv = jnp.einsum('bhnjd,bhnjv->nbhdv', kc * zeta[None, :, None, :, None], vc)

    def step(state, kv_n):
        new_state = state * chunk_decay[None, :, None, None] + kv_n
        return new_state, state

    _, states = lax.scan(step, jnp.zeros((B, H, dk, dv), dtype=kv.dtype), kv)
    cross = jnp.einsum('bhnid,nbhdv->bhniv', qc, states) * xi[None, :, None, :, None]
    return (inner + cross).reshape(B, H, S, dv)


def stick_breaking(q, k, v):
    S, d = q.shape[2], q.shape[3]
    scale = d ** -0.5
    outs = []
    for blk in range(S // SB_BLOCK):
        start = blk * SB_BLOCK
        end = start + SB_BLOCK
        qb = q[:, :, start:end]
        kb = k[:, :, :end]
        vb = v[:, :, :end]
        z = jnp.einsum('bhqd,bhkd->bhqk', qb, kb).astype(jnp.float32) * scale
        t_idx = start + jnp.arange(SB_BLOCK)[:, None]
        s_idx = jnp.arange(end)[None, :]
        mask = s_idx < t_idx
        log1m = jnp.where(mask, jax.nn.log_sigmoid(-z), 0.0)
        between = lax.cumsum(log1m, axis=3, reverse=True) - log1m
        a = jnp.where(mask, jnp.exp(jax.nn.log_sigmoid(z) + between), 0.0)
        outs.append(jnp.einsum('bhqk,bhkd->bhqd', a.astype(v.dtype), vb))
    return jnp.concatenate(outs, axis=2)


def token_mixing(h, w_in, ret_gn_g, sb_norm_g, w_out):
    B, S, _ = h.shape
    split_points = np.cumsum([RET_WIDTH] * 4 + [SB_WIDTH] * 2).tolist()
    rq, rk, rv, rg, sq, sk, sv = jnp.split(h @ w_in, split_points, axis=-1)

    def heads(t, n_heads):
        return t.reshape(B, S, n_heads, HEAD_DIM)

    def to_bhsd(t):
        return jnp.transpose(t, (0, 2, 1, 3))

    ret = retention(to_bhsd(rope(heads(rq, RET_HEADS))), to_bhsd(rope(heads(rk, RET_HEADS))),
                    to_bhsd(heads(rv, RET_HEADS)))
    ret = jnp.transpose(ret, (0, 2, 1, 3)).astype(jnp.float32)
    mu = jnp.mean(ret, axis=-1, keepdims=True)
    var = jnp.mean(jnp.square(ret - mu), axis=-1, keepdims=True)
    ret = ((ret - mu) * lax.rsqrt(var + EPS)).reshape(B, S, RET_WIDTH)
    ret = (ret * ret_gn_g.astype(jnp.float32)).astype(h.dtype) * jax.nn.silu(rg)

    sb = stick_breaking(to_bhsd(heads(sq, SB_HEADS)), to_bhsd(heads(sk, SB_HEADS)),
                        to_bhsd(heads(sv, SB_HEADS)))
    sb = jnp.transpose(sb, (0, 2, 1, 3)).astype(jnp.float32)
    sb = sb * lax.rsqrt(jnp.mean(sb * sb, axis=-1, keepdims=True) + EPS)
    sb = (sb.reshape(B, S, SB_WIDTH) * sb_norm_g.astype(jnp.float32)).astype(h.dtype)

    return jnp.concatenate([ret, sb], axis=-1) @ w_out


def setup_inputs(seed: int = 0) -> dict:
    key = jax.random.key(seed)
    ks = jax.random.split(key, 16)
    f32 = jnp.float32
    nrm = lambda k, shape: jax.random.normal(k, shape, dtype=f32)
    return {
        "x": nrm(ks[0], (BATCH, SEQ, D_MODEL)),
        "c": nrm(ks[1], (BATCH, D_MODEL)),
        "ada_w": nrm(ks[2], (DEPTH, D_MODEL, N_MOD * D_MODEL)) * (0.3 * D_MODEL ** -0.5),
        "ada_b": nrm(ks[3], (DEPTH, N_MOD * D_MODEL)) * 0.01,
        "norm_g": 1.0 + 0.05 * nrm(ks[4], (DEPTH, 3, D_MODEL)),
        "w_in": nrm(ks[5], (DEPTH, D_MODEL, IN_WIDTH)) * D_MODEL ** -0.5,
        "ret_gn_g": 1.0 + 0.05 * nrm(ks[6], (DEPTH, RET_WIDTH)),
        "sb_norm_g": 1.0 + 0.05 * nrm(ks[7], (DEPTH, SB_WIDTH)),
        "w_out": nrm(ks[8], (DEPTH, MIX_WIDTH, D_MODEL)) * MIX_WIDTH ** -0.5,
        "ffn1_w_gu": nrm(ks[9], (DEPTH, D_MODEL, 2 * D_FF)) * D_MODEL ** -0.5,
        "ffn1_w_down": nrm(ks[10], (DEPTH, D_FF, D_MODEL)) * D_FF ** -0.5,
        "ffn2_w_gu": nrm(ks[11], (DEPTH, D_MODEL, 2 * D_FF)) * D_MODEL ** -0.5,
        "ffn2_w_down": nrm(ks[12], (DEPTH, D_FF, D_MODEL)) * D_FF ** -0.5,
        "final_g": 1.0 + 0.05 * nrm(ks[13], (D_MODEL,)),
    }


def reference(x, c, ada_w, ada_b, norm_g, w_in, ret_gn_g, sb_norm_g, w_out,
              ffn1_w_gu, ffn1_w_down, ffn2_w_gu, ffn2_w_down, final_g):
    B = x.shape[0]
    c_act = jax.nn.silu(c)
    for l in range(DEPTH):
        mod = (c_act @ ada_w[l] + ada_b[l]).reshape(B, N_MOD, D_MODEL)
        h = modulate(rmsnorm(x, norm_g[l, 0]), mod[:, 0], mod[:, 1])
        x = x + 0.5 * mod[:, 2][:, None, :] * swiglu(h, ffn1_w_gu[l], ffn1_w_down[l])
        h = modulate(rmsnorm(x, norm_g[l, 1]), mod[:, 3], mod[:, 4])
        x = x + mod[:, 5][:, None, :] * token_mixing(h, w_in[l], ret_gn_g[l], sb_norm_g[l], w_out[l])
        h = modulate(rmsnorm(x, norm_g[l, 2]), mod[:, 6], mod[:, 7])
        x = x + 0.5 * mod[:, 8][:, None, :] * swiglu(h, ffn2_w_gu[l], ffn2_w_down[l])
    return rmsnorm(x, final_g)
```

```python
import functools
import math

import jax
import jax.numpy as jnp
from jax import lax
from jax.experimental import pallas as pl
from jax.experimental.pallas import tpu as pltpu

HEAD_DIM = 128
RET_CHUNK = 128
ROPE_BASE = 10000.0
N_MOD = 9
EPS = 1e-6

V7X_VMEM_BYTES = 64 * 1024 * 1024
VMEM_LIMIT_BYTES = V7X_VMEM_BYTES - 8 * 1024 * 1024
LANES = 128

F32 = jnp.float32
BF16 = jnp.bfloat16


def _params(*semantics):
    return pltpu.CompilerParams(dimension_semantics=semantics, vmem_limit_bytes=VMEM_LIMIT_BYTES)


def _tile(dim, want):
    t = min(dim, want)
    while dim % t:
        t //= 2
    return t


def _ada_kernel(c_ref, w_ref, b_ref, o_ref):
    c = c_ref[...]
    c_act = (c * jax.nn.sigmoid(c)).astype(BF16)
    w = w_ref[0].astype(BF16)
    o_ref[0] = jnp.dot(c_act, w, preferred_element_type=F32) + b_ref[0]


def _ada_modulation(c, ada_w, ada_b):
    depth, d, n = ada_w.shape
    b = c.shape[0]
    rows = -(-b // 8) * 8
    c_pad = jnp.pad(c, ((0, rows - b), (0, 0)))
    tn = _tile(n, 512)
    out = pl.pallas_call(
        _ada_kernel,
        grid=(depth, n // tn),
        in_specs=[
            pl.BlockSpec((rows, d), lambda l, j: (0, 0)),
            pl.BlockSpec((1, d, tn), lambda l, j: (l, 0, j)),
            pl.BlockSpec((1, 1, tn), lambda l, j: (l, 0, j)),
        ],
        out_specs=pl.BlockSpec((1, rows, tn), lambda l, j: (l, 0, j)),
        out_shape=jax.ShapeDtypeStruct((depth, rows, n), F32),
        compiler_params=_params("arbitrary", "arbitrary"),
        name="ada_modulation",
    )(c_pad, ada_w, ada_b.reshape(depth, 1, n))
    return out[:, :b]


def _norm_mod_kernel(x_ref, g_ref, shift_ref, scale_ref, o_ref):
    x = x_ref[0]
    inv = lax.rsqrt(jnp.mean(x * x, axis=-1, keepdims=True) + EPS)
    y = x * inv * g_ref[...]
    o_ref[0] = (y * (1.0 + scale_ref[0]) + shift_ref[0]).astype(o_ref.dtype)


def _norm_mod(x, g, shift, scale):
    b, s, d = x.shape
    ts = _tile(s, 512)
    return pl.pallas_call(
        _norm_mod_kernel,
        grid=(b, s // ts),
        in_specs=[
            pl.BlockSpec((1, ts, d), lambda i, j: (i, j, 0)),
            pl.BlockSpec((1, d), lambda i, j: (0, 0)),
            pl.BlockSpec((1, 1, d), lambda i, j: (i, 0, 0)),
            pl.BlockSpec((1, 1, d), lambda i, j: (i, 0, 0)),
        ],
        out_specs=pl.BlockSpec((1, ts, d), lambda i, j: (i, j, 0)),
        out_shape=jax.ShapeDtypeStruct((b, s, d), BF16),
        compiler_params=_params("arbitrary", "arbitrary"),
        name="norm_mod",
    )(x, g.reshape(1, d), shift.reshape(b, 1, d), scale.reshape(b, 1, d))


def _final_norm_kernel(x_ref, g_ref, o_ref):
    x = x_ref[0]
    inv = lax.rsqrt(jnp.mean(x * x, axis=-1, keepdims=True) + EPS)
    o_ref[0] = x * inv * g_ref[...]


def _final_norm(x, g):
    b, s, d = x.shape
    ts = _tile(s, 512)
    return pl.pallas_call(
        _final_norm_kernel,
        grid=(b, s // ts),
        in_specs=[
            pl.BlockSpec((1, ts, d), lambda i, j: (i, j, 0)),
            pl.BlockSpec((1, d), lambda i, j: (0, 0)),
        ],
        out_specs=pl.BlockSpec((1, ts, d), lambda i, j: (i, j, 0)),
        out_shape=jax.ShapeDtypeStruct((b, s, d), F32),
        compiler_params=_params("arbitrary", "arbitrary"),
        name="final_norm",
    )(x, g.reshape(1, d))


def _swiglu_kernel(h_ref, wg_ref, wu_ref, o_ref):
    h = h_ref[...]
    gate = jnp.dot(h, wg_ref[...], preferred_element_type=F32)
    up = jnp.dot(h, wu_ref[...], preferred_element_type=F32)
    o_ref[...] = (gate * jax.nn.sigmoid(gate) * up).astype(o_ref.dtype)


def _swiglu_up(h, w_gu):
    m, d = h.shape
    f = w_gu.shape[1] // 2
    tm, tn = _tile(m, 1024), _tile(f, 512)
    nf = f // tn
    return pl.pallas_call(
        _swiglu_kernel,
        grid=(m // tm, nf),
        in_specs=[
            pl.BlockSpec((tm, d), lambda i, j: (i, 0)),
            pl.BlockSpec((d, tn), lambda i, j: (0, j)),
            pl.BlockSpec((d, tn), lambda i, j: (0, j + nf)),
        ],
        out_specs=pl.BlockSpec((tm, tn), lambda i, j: (i, j)),
        out_shape=jax.ShapeDtypeStruct((m, f), BF16),
        compiler_params=_params("arbitrary", "arbitrary"),
        name="swiglu_up",
    )(h, w_gu, w_gu)


def _proj_kernel(h_ref, w_ref, o_ref):
    o_ref[...] = jnp.dot(h_ref[...], w_ref[...], preferred_element_type=F32).astype(o_ref.dtype)


def _proj(h, w):
    m, k = h.shape
    n = w.shape[1]
    tm, tn = _tile(m, 1024), _tile(n, 1024)
    return pl.pallas_call(
        _proj_kernel,
        grid=(m // tm, n // tn),
        in_specs=[
            pl.BlockSpec((tm, k), lambda i, j: (i, 0)),
            pl.BlockSpec((k, tn), lambda i, j: (0, j)),
        ],
        out_specs=pl.BlockSpec((tm, tn), lambda i, j: (i, j)),
        out_shape=jax.ShapeDtypeStruct((m, n), BF16),
        compiler_params=_params("arbitrary", "arbitrary"),
        name="proj_in",
    )(h, w)


def _residual_kernel(coef, a_ref, w_ref, x_ref, gate_ref, o_ref):
    y = jnp.dot(a_ref[...], w_ref[...], preferred_element_type=F32)
    o_ref[...] = x_ref[...] + (coef * gate_ref[0]) * y


def _residual_proj(a, w, x, gate, coef, seq):
    m, k = a.shape
    n = w.shape[1]
    b = gate.shape[0]
    tm, tn = _tile(seq, 512), _tile(n, 512)
    rows_per_batch = seq // tm
    return pl.pallas_call(
        functools.partial(_residual_kernel, coef),
        grid=(m // tm, n // tn),
        in_specs=[
            pl.BlockSpec((tm, k), lambda i, j: (i, 0)),
            pl.BlockSpec((k, tn), lambda i, j: (0, j)),
            pl.BlockSpec((tm, tn), lambda i, j: (i, j)),
            pl.BlockSpec((1, 1, tn), lambda i, j: (i // rows_per_batch, 0, j)),
        ],
        out_specs=pl.BlockSpec((tm, tn), lambda i, j: (i, j)),
        out_shape=jax.ShapeDtypeStruct((m, n), F32),
        compiler_params=_params("arbitrary", "arbitrary"),
        name="residual_proj",
    )(a, w, x, gate.reshape(b, 1, n))


def _residual2_kernel(coef, half, a1_ref, a2_ref, w_ref, x_ref, gate_ref, o_ref):
    y = jnp.dot(a1_ref[...], w_ref[:half, :], preferred_element_type=F32)
    y = y + jnp.dot(a2_ref[...], w_ref[half:, :], preferred_element_type=F32)
    o_ref[...] = x_ref[...] + (coef * gate_ref[0]) * y


def _residual_proj2(a1, a2, w, x, gate, coef, seq):
    m, k1 = a1.shape
    k = w.shape[0]
    n = w.shape[1]
    b = gate.shape[0]
    tm, tn = _tile(seq, 1024), _tile(n, 512)
    rows_per_batch = seq // tm
    return pl.pallas_call(
        functools.partial(_residual2_kernel, coef, k1),
        grid=(m // tm, n // tn),
        in_specs=[
            pl.BlockSpec((tm, k1), lambda i, j: (i, 0)),
            pl.BlockSpec((tm, k - k1), lambda i, j: (i, 0)),
            pl.BlockSpec((k, tn), lambda i, j: (0, j)),
            pl.BlockSpec((tm, tn), lambda i, j: (i, j)),
            pl.BlockSpec((1, 1, tn), lambda i, j: (i // rows_per_batch, 0, j)),
        ],
        out_specs=pl.BlockSpec((tm, tn), lambda i, j: (i, j)),
        out_shape=jax.ShapeDtypeStruct((m, n), F32),
        compiler_params=_params("arbitrary", "arbitrary"),
        name="residual_proj2",
    )(a1, a2, w, x, gate.reshape(b, 1, n))


def _ret_kernel(n_chunks, q_ref, k_ref, v_ref, gate_ref, cos_ref, sin_ref, lg_ref, gn_ref, o_ref):
    c = RET_CHUNK
    lg = lg_ref[0]
    row = lax.broadcasted_iota(jnp.int32, (c, LANES), 0).astype(F32)
    col = lax.broadcasted_iota(jnp.int32, (c, LANES), 1).astype(F32)
    diff = row - col
    dmask = jnp.where(diff >= 0, jnp.exp(lg * jnp.maximum(diff, 0.0)), 0.0)
    xi = jnp.exp(lg * (row + 1.0))
    zeta = jnp.exp(lg * (c - 1.0 - row))
    chunk_decay = jnp.exp(lg * float(c))
    k_scale = HEAD_DIM ** -0.5
    gn = gn_ref[...]

    def rope(t, cos, sin):
        return t * cos + pltpu.roll(t, HEAD_DIM // 2, 1) * sin

    def body(n, state):
        rows = pl.ds(pl.multiple_of(n * c, c), c)
        cos = cos_ref[rows, :]
        sin = sin_ref[rows, :]
        q = rope(q_ref[0, rows, :].astype(F32), cos, sin).astype(BF16)
        k = rope(k_ref[0, rows, :].astype(F32), cos, sin) * k_scale
        v = v_ref[0, rows, :]
        scores = lax.dot_general(q, k.astype(BF16), (((1,), (1,)), ((), ())),
                                 preferred_element_type=F32) * dmask
        inner = jnp.dot(scores.astype(BF16), v, preferred_element_type=F32)
        cross = jnp.dot(q, state.astype(BF16), preferred_element_type=F32) * xi
        kz_t = (k * zeta).T.astype(BF16)
        new_state = state * chunk_decay + jnp.dot(kz_t, v, preferred_element_type=F32)
        ret = inner + cross
        mu = jnp.mean(ret, axis=-1, keepdims=True)
        cen = ret - mu
        var = jnp.mean(cen * cen, axis=-1, keepdims=True)
        y = cen * lax.rsqrt(var + EPS) * gn
        g = gate_ref[0, rows, :].astype(F32)
        o_ref[0, rows, :] = (y * (g * jax.nn.sigmoid(g))).astype(o_ref.dtype)
        return new_state

    lax.fori_loop(0, n_chunks, body, jnp.zeros((HEAD_DIM, HEAD_DIM), F32))


def _retention_heads(qkv, n_heads, gn_g, cos_t, sin_t):
    b, s, _ = qkv.shape
    d = HEAD_DIM
    lg = jnp.log1p(-jnp.exp2(-5.0 - jnp.arange(n_heads, dtype=F32)))
    lg_t = jnp.broadcast_to(lg[:, None, None], (n_heads, 1, LANES))
    seq_spec = lambda off: pl.BlockSpec((1, s, d), lambda i, h: (i, 0, h + off))
    return pl.pallas_call(
        functools.partial(_ret_kernel, s // RET_CHUNK),
        grid=(b, n_heads),
        in_specs=[
            seq_spec(0), seq_spec(n_heads), seq_spec(2 * n_heads), seq_spec(3 * n_heads),
            pl.BlockSpec((s, d), lambda i, h: (0, 0)),
            pl.BlockSpec((s, d), lambda i, h: (0, 0)),
            pl.BlockSpec((1, 1, LANES), lambda i, h: (h, 0, 0)),
            pl.BlockSpec((1, d), lambda i, h: (0, h)),
        ],
        out_specs=pl.BlockSpec((1, s, d), lambda i, h: (i, 0, h)),
        out_shape=jax.ShapeDtypeStruct((b, s, n_heads * d), BF16),
        compiler_params=_params("arbitrary", "arbitrary"),
        name="retention",
    )(qkv, qkv, qkv, qkv, cos_t, sin_t, lg_t, gn_g.reshape(1, n_heads * d))


def _sb_kernel(tq, tk, q_ref, k_ref, v_ref, g_ref, o_ref):
    qi = pl.program_id(2)
    q = q_ref[0]
    scale = HEAD_DIM ** -0.5
    n_diag = tq // tk

    r = lax.broadcasted_iota(jnp.int32, (tk, 2 * tk), 0)
    cc = lax.broadcasted_iota(jnp.int32, (tk, 2 * tk), 1)
    suffix = jnp.where((cc >= tk) | (r > cc), 1.0, 0.0).astype(BF16)

    t_idx = qi * tq + lax.broadcasted_iota(jnp.int32, (tq, tk), 0)
    s_loc = lax.broadcasted_iota(jnp.int32, (tq, tk), 1)

    def block(j, carry, acc, masked):
        rows = pl.ds(pl.multiple_of(j * tk, tk), tk)
        kj = k_ref[0, rows, :]
        vj = v_ref[0, rows, :]
        z = lax.dot_general(q, kj, (((1,), (1,)), ((), ())), preferred_element_type=F32) * scale
        log_sig = jnp.minimum(z, 0.0) - jnp.log1p(jnp.exp(-jnp.abs(z)))
        log_1m = log_sig - z
        if masked:
            mask = (j * tk + s_loc) < t_idx
            log_1m = jnp.where(mask, log_1m, 0.0)
        hi = log_1m.astype(BF16)
        lo = (log_1m - hi.astype(F32)).astype(BF16)
        sums = (jnp.dot(hi, suffix, preferred_element_type=F32)
                + jnp.dot(lo, suffix, preferred_element_type=F32))
        between = carry + sums[:, :tk]
        a = jnp.exp(log_sig + between)
        if masked:
            a = jnp.where(mask, a, 0.0)
        acc = acc + jnp.dot(a.astype(BF16), vj, preferred_element_type=F32)
        return carry + sums[:, tk:], acc

    carry = jnp.zeros((tq, tk), F32)
    acc = jnp.zeros((tq, HEAD_DIM), F32)
    first_full = qi * n_diag
    for dblk in reversed(range(n_diag)):
        carry, acc = block(first_full + dblk, carry, acc, True)

    def body(t, state):
        return block(first_full - 1 - t, state[0], state[1], False)

    carry, acc = lax.fori_loop(0, first_full, body, (carry, acc))
    inv = lax.rsqrt(jnp.mean(acc * acc, axis=-1, keepdims=True) + EPS)
    o_ref[0] = (acc * inv * g_ref[...]).astype(o_ref.dtype)


def _stick_breaking_heads(qkv, col_off, n_heads, norm_g):
    b, s, _ = qkv.shape
    d = HEAD_DIM
    tq, tk = _tile(s, 128), 128
    kv_spec = lambda off: pl.BlockSpec((1, s, d), lambda i, h, t: (i, 0, col_off + off + h))
    return pl.pallas_call(
        functools.partial(_sb_kernel, tq, tk),
        grid=(b, n_heads, s // tq),
        in_specs=[
            pl.BlockSpec((1, tq, d), lambda i, h, t: (i, t, col_off + h)),
            kv_spec(n_heads), kv_spec(2 * n_heads),
            pl.BlockSpec((1, d), lambda i, h, t: (0, h)),
        ],
        out_specs=pl.BlockSpec((1, tq, d), lambda i, h, t: (i, t, h)),
        out_shape=jax.ShapeDtypeStruct((b, s, n_heads * d), BF16),
        compiler_params=_params("arbitrary", "arbitrary", "arbitrary"),
        name="stick_breaking",
    )(qkv, qkv, qkv, norm_g.reshape(1, n_heads * d))


def _rope_tables(s):
    half = HEAD_DIM // 2
    inv = ROPE_BASE ** (-jnp.arange(half, dtype=F32) / half)
    ang = jnp.arange(s, dtype=F32)[:, None] * inv[None, :]
    cos, sin = jnp.cos(ang), jnp.sin(ang)
    return jnp.concatenate([cos, cos], axis=-1), jnp.concatenate([-sin, sin], axis=-1)


def kernel(x, c, ada_w, ada_b, norm_g, w_in, ret_gn_g, sb_norm_g, w_out, ffn1_w_gu, ffn1_w_down,
           ffn2_w_gu, ffn2_w_down, final_g):
    b, s, d = x.shape
    depth = ada_w.shape[0]
    ret_heads = ret_gn_g.shape[1] // HEAD_DIM
    sb_heads = sb_norm_g.shape[1] // HEAD_DIM
    assert w_in.shape[2] == (4 * ret_heads + 3 * sb_heads) * HEAD_DIM
    m = b * s

    mod_all = _ada_modulation(c, ada_w, ada_b).reshape(depth, b, N_MOD, d)
    cos_t, sin_t = _rope_tables(s)

    def ffn(x, g, shift, scale, gate, w_gu, w_down):
        h = _norm_mod(x, g, shift, scale).reshape(m, d)
        act = _swiglu_up(h, w_gu.astype(BF16))
        return _residual_proj(act, w_down.astype(BF16), x.reshape(m, d), gate, 0.5, s).reshape(b, s, d)

    for l in range(depth):
        mod = mod_all[l]
        x = ffn(x, norm_g[l, 0], mod[:, 0], mod[:, 1], mod[:, 2], ffn1_w_gu[l], ffn1_w_down[l])

        h = _norm_mod(x, norm_g[l, 1], mod[:, 3], mod[:, 4]).reshape(m, d)
        qkv = _proj(h, w_in[l].astype(BF16)).reshape(b, s, -1)
        ret = _retention_heads(qkv, ret_heads, ret_gn_g[l], cos_t, sin_t)
        sb = _stick_breaking_heads(qkv, 4 * ret_heads, sb_heads, sb_norm_g[l])
        x = _residual_proj2(ret.reshape(m, -1), sb.reshape(m, -1), w_out[l].astype(BF16),
                            x.reshape(m, d), mod[:, 5], 1.0, s).reshape(b, s, d)

        x = ffn(x, norm_g[l, 2], mod[:, 6], mod[:, 7], mod[:, 8], ffn2_w_gu[l], ffn2_w_down[l])
    return _final_norm(x, final_g)
```
